```python
import math
import jax, jax.numpy as jnp
from jax import lax
import numpy as np

D_MODEL = 1024
BATCH = 2
SEQ = 8192
DEPTH = 1
DEC_BATCH = 128
DEC_SEQ = 4
PAST_LEN = 2048
PAGE_SIZE = 128

N_META = 16
ATT_WIDTH = D_MODEL // 2
POOL_WIDTH = D_MODEL - ATT_WIDTH
N_ATT_HEADS = 4
HEAD_DIM = ATT_WIDTH // N_ATT_HEADS
QK_DIM = HEAD_DIM // 2
ROPE_DIM = QK_DIM // 4
ROPE_THETA = 500000.0
POOL_WINDOWS = (2, 4, 8, 16)
N_POOL_GROUPS = len(POOL_WINDOWS)
POOL_GROUP_DIM = POOL_WIDTH // N_POOL_GROUPS
POOL_STATE = max(POOL_WINDOWS) - 1
IN_WIDTH = 3 * ATT_WIDTH + POOL_WIDTH
N_KEYS = 128
N_EXPERTS = N_KEYS * N_KEYS
PEER_HEADS = 8
PEER_QDIM = 256
PEER_HALF = PEER_QDIM // 2
PEER_TOPK = 16
PEER_BLOCK = 128
Q_BLOCK = 128
EPS = 1e-6
NEG = -1e30

kernel_name = 'hymba_diffattn_pool_peer_step'


def rmsnorm(x, g):
    xf = x.astype(jnp.float32)
    y = xf * lax.rsqrt(jnp.mean(xf * xf, axis=-1, keepdims=True) + EPS)
    return (y * g.astype(jnp.float32)).astype(x.dtype)


def rope(x, pos):
    inv = ROPE_THETA ** (-jnp.arange(0, ROPE_DIM, 2, dtype=jnp.float32) / ROPE_DIM)
    ang = pos.astype(jnp.float32)[:, None] * inv[None, :]
    cos = jnp.cos(ang)[:, None, None, :]
    sin = jnp.sin(ang)[:, None, None, :]
    xr = x[..., :ROPE_DIM].astype(jnp.float32)
    x1, x2 = xr[..., :ROPE_DIM // 2], xr[..., ROPE_DIM // 2:]
    rot = jnp.concatenate([x1 * cos - x2 * sin, x2 * cos + x1 * sin], axis=-1)
    return jnp.concatenate([rot.astype(x.dtype), x[..., ROPE_DIM:]], axis=-1)


def diff_attn(q, k, v, q_pos, k_pos, lam):
    s = jnp.einsum('bqhcd,bkhcd->bchqk', q, k).astype(jnp.float32) * (QK_DIM ** -0.5)
    mask = k_pos[None, :] <= q_pos[:, None]
    s = jnp.where(mask, s, NEG)
    p = jax.nn.softmax(s, axis=-1)
    a = p[:, 0] - lam * p[:, 1]
    return jnp.einsum('bhqk,bkhd->bqhd', a.astype(v.dtype), v)


def attn_blocks(q, k, v, pos, lam):
    B, T = q.shape[0], q.shape[1]
    nb = -(-T // Q_BLOCK)
    tp = nb * Q_BLOCK
    qp = jnp.pad(q, ((0, 0), (0, tp - T), (0, 0), (0, 0), (0, 0)))
    qpos = jnp.arange(tp)

    def blk(i):
        st = i * Q_BLOCK
        qb = lax.dynamic_slice_in_dim(qp, st, Q_BLOCK, axis=1)
        pb = lax.dynamic_slice_in_dim(qpos, st, Q_BLOCK)
        return diff_attn(qb, k, v, pb, pos, lam)

    o = lax.map(blk, jnp.arange(nb))
    o = jnp.moveaxis(o, 0, 1).reshape(B, tp, N_ATT_HEADS, HEAD_DIM)
    return o[:, :T]


def pool_mix(p, prev, pos, w_pool, b_pool, pool_scale):
    B, T, _ = p.shape
    ext = jnp.concatenate([prev.astype(p.dtype), p], axis=1)
    cz = jnp.pad(jnp.cumsum(ext.astype(jnp.float32), axis=1), ((0, 0), (1, 0), (0, 0)))
    means = []
    for g, w in enumerate(POOL_WINDOWS):
        c0, c1 = g * POOL_GROUP_DIM, (g + 1) * POOL_GROUP_DIM
        tot = cz[:, POOL_STATE + 1:POOL_STATE + 1 + T, c0:c1] - cz[:, POOL_STATE + 1 - w:POOL_STATE + 1 - w + T, c0:c1]
        cnt = jnp.minimum(pos + 1, w).astype(jnp.float32)[None, :, None]
        means.append(tot / cnt)
    d = (jnp.concatenate(means, axis=-1) - p.astype(jnp.float32)).astype(p.dtype)
    d = d.reshape(B, T, N_POOL_GROUPS, POOL_GROUP_DIM)
    y = jnp.einsum('btgc,gcd->btgd', d, w_pool) + b_pool
    y = y.reshape(B, T, POOL_WIDTH) * pool_scale
    return y, ext[:, -POOL_STATE:]


def peer(x, w_query, sub_keys, w_u, w_v):
    B, T, D = x.shape
    n = B * T
    nb = -(-n // PEER_BLOCK)
    xf = jnp.pad(x.reshape(n, D), ((0, nb * PEER_BLOCK - n), (0, 0))).reshape(nb, PEER_BLOCK, D)

    def blk(xb):
        q = (xb @ w_query).reshape(PEER_BLOCK, PEER_HEADS, 2, PEER_HALF)
        s = jnp.einsum('thcd,hcnd->thcn', q, sub_keys).astype(jnp.float32)
        sv, si = lax.top_k(s, PEER_TOPK)
        cand = (sv[:, :, 0, :, None] + sv[:, :, 1, None, :]).reshape(PEER_BLOCK, PEER_HEADS, PEER_TOPK * PEER_TOPK)
        cidx = (si[:, :, 0, :, None] * N_KEYS + si[:, :, 1, None, :]).reshape(PEER_BLOCK, PEER_HEADS, PEER_TOPK * PEER_TOPK)
        top, sel = lax.top_k(cand, PEER_TOPK)
        eidx = jnp.take_along_axis(cidx, sel, axis=-1)
        gate = jax.nn.softmax(top, axis=-1)
        a = jnp.einsum('td,thkd->thk', xb, w_u[eidx])
        a = jax.nn.gelu(a.astype(jnp.float32), approximate=False) * gate
        return jnp.einsum('thk,thkd->td', a.astype(xb.dtype), w_v[eidx])

    y = lax.map(blk, xf).reshape(nb * PEER_BLOCK, D)[:n]
    return y.reshape(B, T, D)


def hybrid_layer(x, pos, pool_prev, past_k, past_v, lam, lam_init, norm1_g, w_in, q_norm_g, k_norm_g,
                 subln_g, w_pool, b_pool, pool_scale, w_out, norm2_g, w_query, sub_keys, w_u, w_v):
    B, T, _ = x.shape
    h = rmsnorm(x, norm1_g)
    z = h @ w_in
    q, k, v, p = jnp.split(z, [ATT_WIDTH, 2 * ATT_WIDTH, 3 * ATT_WIDTH], axis=-1)
    q = rope(rmsnorm(q.reshape(B, T, N_ATT_HEADS, 2, QK_DIM), q_norm_g), pos)
    k = rope(rmsnorm(k.reshape(B, T, N_ATT_HEADS, 2, QK_DIM), k_norm_g), pos)
    v = v.reshape(B, T, N_ATT_HEADS, HEAD_DIM)
    if past_k is None:
        att = attn_blocks(q, k, v, pos, lam)
    else:
        n_past = past_k.shape[1]
        k_all = jnp.concatenate([past_k.astype(k.dtype), k], axis=1)
        v_all = jnp.concatenate([past_v.astype(v.dtype), v], axis=1)
        att = diff_attn(q, k_all, v_all, pos, jnp.arange(n_past + T), lam)
    att = rmsnorm(att, subln_g) * (1.0 - lam_init)
    pool_y, pool_new = pool_mix(p, pool_prev, pos, w_pool, b_pool, pool_scale)
    x = x + jnp.concatenate([att.reshape(B, T, ATT_WIDTH), pool_y], axis=-1) @ w_out
    x = x + peer(rmsnorm(x, norm2_g), w_query, sub_keys, w_u, w_v)
    return x, k.reshape(B, T, N_ATT_HEADS, 2 * QK_DIM), v, pool_new


def setup_inputs(seed: int = 0) -> dict:
    key = jax.random.key(seed)
    ks = jax.random.split(key, 26)
    f32 = jnp.float32
    n_pages = PAST_LEN // PAGE_SIZE
    n_pool = (DEC_BATCH * n_pages * 5) // 4
    nrm = lambda k, s, sc: jax.random.normal(k, s, f32) * sc
    gain = lambda k, s: 1.0 + 0.05 * jax.random.normal(k, s, f32)
    perm = jax.random.permutation(ks[5], n_pool)[:DEC_BATCH * n_pages]
    return {
        'x_prompt': nrm(ks[0], (BATCH, SEQ, D_MODEL), 1.0),
        'x_sample': nrm(ks[1], (DEC_BATCH, DEC_SEQ, D_MODEL), 1.0),
        'cache_k': nrm(ks[2], (DEPTH, n_pool, PAGE_SIZE, N_ATT_HEADS, 2 * QK_DIM), 1.0),
        'cache_v': nrm(ks[3], (DEPTH, n_pool, PAGE_SIZE, N_ATT_HEADS, HEAD_DIM), 1.0),
        'state_pool': nrm(ks[4], (DEPTH, DEC_BATCH, POOL_STATE, POOL_WIDTH), 1.0),
        'page_table': perm.reshape(DEC_BATCH, n_pages).astype(jnp.int32),
        'meta_tokens': nrm(ks[6], (N_META, D_MODEL), 1.0),
        'norm1_g': gain(ks[7], (DEPTH, D_MODEL)),
        'w_in': nrm(ks[8], (DEPTH, D_MODEL, IN_WIDTH), D_MODEL ** -0.5),
        'q_norm_g': gain(ks[9], (DEPTH, QK_DIM)),
        'k_norm_g': gain(ks[10], (DEPTH, QK_DIM)),
        'lambda_q1': nrm(ks[11], (DEPTH, QK_DIM), 0.1),
        'lambda_k1': nrm(ks[12], (DEPTH, QK_DIM), 0.1),
        'lambda_q2': nrm(ks[13], (DEPTH, QK_DIM), 0.1),
        'lambda_k2': nrm(ks[14], (DEPTH, QK_DIM), 0.1),
        'subln_g': gain(ks[15], (DEPTH, HEAD_DIM)),
        'w_pool': nrm(ks[16], (DEPTH, N_POOL_GROUPS, POOL_GROUP_DIM, POOL_GROUP_DIM), POOL_GROUP_DIM ** -0.5),
        'b_pool': nrm(ks[17], (DEPTH, N_POOL_GROUPS, POOL_GROUP_DIM), 0.02),
        'pool_scale': gain(ks[18], (DEPTH, POOL_WIDTH)),
        'w_out': nrm(ks[19], (DEPTH, D_MODEL, D_MODEL), D_MODEL ** -0.5),
        'norm2_g': gain(ks[20], (DEPTH, D_MODEL)),
        'w_query': nrm(ks[21], (DEPTH, D_MODEL, PEER_HEADS * PEER_QDIM), D_MODEL ** -0.5),
        'sub_keys': nrm(ks[22], (DEPTH, PEER_HEADS, 2, N_KEYS, PEER_HALF), PEER_HALF ** -0.5),
        'w_u': nrm(ks[23], (DEPTH, N_EXPERTS, D_MODEL), D_MODEL ** -0.5),
        'w_v': nrm(ks[24], (DEPTH, N_EXPERTS, D_MODEL), 0.3),
    }


def reference(x_prompt, x_sample, cache_k, cache_v, state_pool, page_table, meta_tokens, norm1_g, w_in,
              q_norm_g, k_norm_g, lambda_q1, lambda_k1, lambda_q2, lambda_k2, subln_g, w_pool, b_pool,
              pool_scale, w_out, norm2_g, w_query, sub_keys, w_u, w_v):
    B = x_prompt.shape[0]
    DB, S = x_sample.shape[0], x_sample.shape[1]
    n_past = page_table.shape[1] * PAGE_SIZE
    xp = jnp.concatenate([jnp.broadcast_to(meta_tokens[None].astype(x_prompt.dtype), (B, N_META, D_MODEL)), x_prompt], axis=1)
    xs = x_sample
    pos_p = jnp.arange(xp.shape[1])
    pos_s = n_past + jnp.arange(S)
    pk, pv, pp, sk, sv, sp = [], [], [], [], [], []
    for l in range(DEPTH):
        lam_init = 0.8 - 0.6 * math.exp(-0.3 * l)
        lam = (jnp.exp(jnp.sum(lambda_q1[l].astype(jnp.float32) * lambda_k1[l].astype(jnp.float32)))
               - jnp.exp(jnp.sum(lambda_q2[l].astype(jnp.float32) * lambda_k2[l].astype(jnp.float32))) + lam_init)
        w = (norm1_g[l], w_in[l], q_norm_g[l], k_norm_g[l], subln_g[l], w_pool[l], b_pool[l], pool_scale[l],
             w_out[l], norm2_g[l], w_query[l], sub_keys[l], w_u[l], w_v[l])
        zero_prev = jnp.zeros((B, POOL_STATE, POOL_WIDTH), xp.dtype)
        xp, k_new, v_new, pool_new = hybrid_layer(xp, pos_p, zero_prev, None, None, lam, lam_init, *w)
        pk.append(k_new); pv.append(v_new); pp.append(pool_new)
        past_k = cache_k[l][page_table].reshape(DB, n_past, N_ATT_HEADS, 2, QK_DIM)
        past_v = cache_v[l][page_table].reshape(DB, n_past, N_ATT_HEADS, HEAD_DIM)
        xs, k_new, v_new, pool_new = hybrid_layer(xs, pos_s, state_pool[l], past_k, past_v, lam, lam_init, *w)
        sk.append(k_new); sv.append(v_new); sp.append(pool_new)
    y_prompt = xp[:, N_META:]
    y_sample = xs
    return (y_prompt, y_sample, jnp.stack(pk, 0), jnp.stack(pv, 0), jnp.stack(pp, 0),
            jnp.stack(sk, 0), jnp.stack(sv, 0), jnp.stack(sp, 0))
```

```python
import functools
import math

import numpy as np
import jax
import jax.numpy as jnp
from jax import lax
from jax.experimental import pallas as pl
from jax.experimental.pallas import tpu as pltpu

F32 = jnp.float32
BF16 = jnp.bfloat16

N_META = 16
N_ATT_HEADS = 4
HEAD_DIM = 128
QK_DIM = 64
ROPE_DIM = 16
ROPE_THETA = 500000.0
POOL_WINDOWS = (2, 4, 8, 16)
POOL_GROUP_DIM = 128
POOL_STATE = 15
N_KEYS = 128
PEER_HEADS = 8
PEER_HALF = 128
PEER_TOPK = 16
EPS = 1e-6
NEG = -1e30

LANES = 128
SUBLANES = 8
MXU_DIM = 256
VMEM_LIMIT_BYTES = 56 * 1024 * 1024

_ERF_ALPHA = (-2.72614225801306e-10, 2.77068142495902e-08, -2.10102402082508e-06,
              -5.69250639462346e-05, -7.34990630326855e-04, -2.95459980854025e-03,
              -1.60960333262415e-02)
_ERF_BETA = (-1.45660718464996e-05, -2.13374055278905e-04, -1.68282697438203e-03,
             -7.37332916720468e-03, -1.42647390514189e-02)


def _erf(x):
    x = jnp.clip(x, -4.0, 4.0)
    x2 = x * x
    num = jnp.full_like(x, _ERF_ALPHA[0])
    for c in _ERF_ALPHA[1:]:
        num = num * x2 + c
    den = jnp.full_like(x, _ERF_BETA[0])
    for c in _ERF_BETA[1:]:
        den = den * x2 + c
    return x * num / den


def _gelu(x):
    return 0.5 * x * (1.0 + _erf(x * (2.0 ** -0.5)))


def _cparams(*sem):
    return pltpu.CompilerParams(dimension_semantics=sem, vmem_limit_bytes=VMEM_LIMIT_BYTES)


def _dot(a, b):
    return jnp.dot(a, b, preferred_element_type=F32)


def _dot_t(a, b):
    return lax.dot_general(a, b, (((1,), (1,)), ((), ())), preferred_element_type=F32)


def _rope_tables(positions):
    pos = np.asarray(positions, np.float64)[:, None]
    half = ROPE_DIM // 2
    inv = ROPE_THETA ** (-np.arange(0, ROPE_DIM, 2, dtype=np.float64) / ROPE_DIM)
    ang = pos * inv[None, :]
    cos, sin = np.cos(ang), np.sin(ang)
    n = pos.shape[0]
    c = np.ones((n, LANES)); s1 = np.zeros((n, LANES)); s2 = np.zeros((n, LANES))
    for comp in range(LANES // QK_DIM):
        o = comp * QK_DIM
        c[:, o:o + half] = cos; c[:, o + half:o + ROPE_DIM] = cos
        s2[:, o:o + half] = -sin
        s1[:, o + half:o + ROPE_DIM] = sin
    return (jnp.asarray(c, F32), jnp.asarray(s1, F32), jnp.asarray(s2, F32))


def _proj_kernel(x_ref, g1_ref, win_ref, bd_ref, gq_ref, gk_ref, c_ref, s1_ref, s2_ref,
                 k_ref, v_ref, q0_ref, q1_ref, kb_ref, vb_ref, p_ref):
    aw = N_ATT_HEADS * HEAD_DIM
    x = x_ref[...]
    ms = jnp.mean(x * x, axis=-1, keepdims=True)
    h = (x * lax.rsqrt(ms + EPS) * g1_ref[...]).astype(BF16)
    z = _dot(h, win_ref[...])
    c = jnp.concatenate([c_ref[...]] * N_ATT_HEADS, axis=1)
    s1 = jnp.concatenate([s1_ref[...]] * N_ATT_HEADS, axis=1)
    s2 = jnp.concatenate([s2_ref[...]] * N_ATT_HEADS, axis=1)
    bd = bd_ref[...]

    def qk_norm_rope(u, g):
        sq = u * u
        hi = sq.astype(BF16)
        lo = (sq - hi.astype(F32)).astype(BF16)
        msq = _dot(hi, bd) + _dot(lo, bd)
        un = u * lax.rsqrt(msq + EPS) * g
        return un * c + pltpu.roll(un, ROPE_DIM // 2, 1) * s1 + pltpu.roll(un, aw - ROPE_DIM // 2, 1) * s2

    q = qk_norm_rope(z[:, :aw], gq_ref[...]) * (QK_DIM ** -0.5)
    k = qk_norm_rope(z[:, aw:2 * aw], gk_ref[...])
    v = z[:, 2 * aw:3 * aw]
    lane = lax.broadcasted_iota(jnp.int32, q.shape, 1)
    first = (lane % (2 * QK_DIM)) < QK_DIM
    q0_ref[...] = jnp.where(first, q, 0.0).astype(BF16)
    q1_ref[...] = jnp.where(first, 0.0, q).astype(BF16)
    k_ref[...] = k
    v_ref[...] = v
    kb_ref[...] = k.astype(BF16)
    vb_ref[...] = v.astype(BF16)
    p_ref[...] = z[:, 3 * aw:]


def _project(x3, n_valid, rb, tabs, g1, win, bd, gq, gk):
    nb, r, d = x3.shape
    aw = N_ATT_HEADS * HEAD_DIM
    in_w = win.shape[1]
    row = lambda b, j: (b, j, 0)
    fixed = lambda b, j: (0, 0)
    blk = lambda w: pl.BlockSpec((None, rb, w), row)
    full = lambda a: pl.BlockSpec(a.shape, fixed)
    tab = pl.BlockSpec((rb, LANES), lambda b, j: (j, 0))
    return pl.pallas_call(
        _proj_kernel,
        grid=(nb, r // rb),
        in_specs=[blk(d), full(g1), full(win), full(bd), full(gq), full(gk), tab, tab, tab],
        out_specs=[blk(aw)] * 7,
        out_shape=[jax.ShapeDtypeStruct((nb, n_valid, aw), F32)] * 2
        + [jax.ShapeDtypeStruct((nb, r, aw), BF16)] * 4
        + [jax.ShapeDtypeStruct((nb, r, in_w - 3 * aw), F32)],
        compiler_params=_cparams("parallel", "arbitrary"),
        name="proj",
    )(x3, g1, win, bd, gq, gk, *tabs)


def _lambda(lq1, lk1, lq2, lk2, lam_init):
    a = jnp.sum(lq1[...] * lk1[...], axis=-1, keepdims=True)
    b = jnp.sum(lq2[...] * lk2[...], axis=-1, keepdims=True)
    return jnp.exp(a) - jnp.exp(b) + lam_init


def _subln(att, g, lam_init):
    ms = jnp.mean(att * att, axis=-1, keepdims=True)
    return att * lax.rsqrt(ms + EPS) * g * (1.0 - lam_init)


def _flash_kernel(q0_ref, q1_ref, k_ref, v_ref, lq1, lk1, lq2, lk2, g_ref, o_ref, m_ref, acc_ref,
                  *, tq, lam_init):
    qi = pl.program_id(2)
    qs = jnp.concatenate([q0_ref[...], q1_ref[...]], axis=0)
    m_ref[...] = jnp.full(m_ref.shape, NEG, F32)
    acc_ref[...] = jnp.zeros(acc_ref.shape, F32)
    ones = jnp.ones((tq, HEAD_DIM), BF16)

    def step(j, masked):
        start = pl.multiple_of(j * tq, tq)
        kc = k_ref[pl.ds(start, tq), :]
        vc = jnp.concatenate([v_ref[pl.ds(start, tq), :], ones], axis=1)
        s = _dot_t(qs, kc)
        if masked:
            row = lax.broadcasted_iota(jnp.int32, s.shape, 0) % tq
            col = lax.broadcasted_iota(jnp.int32, s.shape, 1)
            s = jnp.where(col <= row, s, NEG)
        m_prev = m_ref[...]
        m_new = jnp.maximum(m_prev, jnp.max(s, axis=1, keepdims=True))
        alpha = jnp.exp(m_prev - m_new)
        p = jnp.exp(s - jnp.concatenate([m_new] * (tq // LANES), axis=1))
        acc_ref[...] = (acc_ref[...] * jnp.concatenate([alpha] * 2, axis=1)
                        + _dot(p.astype(BF16), vc))
        m_ref[...] = m_new

    def body(j, carry):
        step(j, False)
        return carry

    lax.fori_loop(0, qi, body, 0)
    step(qi, True)

    acc = acc_ref[...]
    o = acc[:, :HEAD_DIM] / acc[:, HEAD_DIM:]
    lam = _lambda(lq1, lk1, lq2, lk2, lam_init)
    att = o[:tq] - lam * o[tq:]
    o_ref[...] = _subln(att, g_ref[...], lam_init).astype(BF16)


def _flash(q0, q1, kb, vb, lams, subln_g, tq, lam_init):
    nb, tp, aw = q0.shape
    qspec = pl.BlockSpec((None, tq, HEAD_DIM), lambda b, h, i: (b, i, h))
    kspec = pl.BlockSpec((None, tp, HEAD_DIM), lambda b, h, i: (b, 0, h))
    small = lambda a: pl.BlockSpec(a.shape, lambda b, h, i: (0, 0))
    return pl.pallas_call(
        functools.partial(_flash_kernel, tq=tq, lam_init=lam_init),
        grid=(nb, N_ATT_HEADS, tp // tq),
        in_specs=[qspec, qspec, kspec, kspec] + [small(a) for a in lams] + [small(subln_g)],
        out_specs=qspec,
        out_shape=jax.ShapeDtypeStruct((nb, tp, aw), BF16),
        scratch_shapes=[pltpu.VMEM((2 * tq, LANES), F32), pltpu.VMEM((2 * tq, 2 * HEAD_DIM), F32)],
        compiler_params=_cparams("parallel", "parallel", "arbitrary"),
        name="prompt_attention",
    )(q0, q1, kb, vb, *lams, subln_g)


def _sample_attn_kernel(pt_ref, *refs, n_pages, s_new, lam_init):
    del pt_ref
    q_ref, kn_ref, vn_ref, lq1, lk1, lq2, lk2, g_ref = refs[:8]
    kp = refs[8:8 + n_pages]
    vp = refs[8 + n_pages:8 + 2 * n_pages]
    o_ref = refs[8 + 2 * n_pages]
    aw = N_ATT_HEADS * HEAD_DIM
    nrow = 2 * N_ATT_HEADS * s_new
    row = lax.broadcasted_iota(jnp.int32, (nrow, aw), 0)
    lane = lax.broadcasted_iota(jnp.int32, (nrow, aw), 1)
    head = (row % (N_ATT_HEADS * s_new)) // s_new
    comp = row // (N_ATT_HEADS * s_new)
    qbd = jnp.where(lane // QK_DIM == 2 * head + comp, q_ref[...], 0.0).astype(BF16)
    page = kp[0].shape[0]
    pad = jnp.zeros((page - kn_ref.shape[0], aw), BF16)
    kn = jnp.concatenate([kn_ref[...], pad], axis=0)
    vn = jnp.concatenate([vn_ref[...], pad], axis=0)
    s_list = [_dot_t(qbd, kp[j][...].astype(BF16)) for j in range(n_pages)]
    r2 = lax.broadcasted_iota(jnp.int32, (nrow, page), 0) % s_new
    c2 = lax.broadcasted_iota(jnp.int32, (nrow, page), 1)
    s_list.append(jnp.where(c2 <= r2, _dot_t(qbd, kn), NEG))
    m = s_list[0]
    for s in s_list[1:]:
        m = jnp.maximum(m, s)
    m = jnp.max(m, axis=1, keepdims=True)
    den = jnp.zeros((nrow, 1), F32)
    acc = jnp.zeros((nrow, aw), F32)
    for j, s in enumerate(s_list):
        p = jnp.exp(s - m)
        den = den + jnp.sum(p, axis=1, keepdims=True)
        vj = vn if j == n_pages else vp[j][...].astype(BF16)
        acc = acc + _dot(p.astype(BF16), vj)
    o = acc / den
    half = N_ATT_HEADS * s_new
    lam = _lambda(lq1, lk1, lq2, lk2, lam_init)
    att = o[:half] - lam * o[half:]
    own = (lax.broadcasted_iota(jnp.int32, (half, aw), 0) // s_new
           == lax.broadcasted_iota(jnp.int32, (half, aw), 1) // HEAD_DIM)
    att = jnp.where(own, att, 0.0)
    ms = jnp.sum(att * att, axis=-1, keepdims=True) * (1.0 / HEAD_DIM)
    att = att * lax.rsqrt(ms + EPS) * g_ref[...] * (1.0 - lam_init)
    out = att[:s_new]
    for hh in range(1, N_ATT_HEADS):
        out = out + att[hh * s_new:(hh + 1) * s_new]
    o_ref[...] = out


def _sample_attention(qrep, kn, vn, ck, cv, page_table, lams, subln_g4, s_new, lam_init):
    db, nrow, aw = qrep.shape
    n_pages = page_table.shape[1]
    page = ck.shape[1]
    seq = lambda a: pl.BlockSpec((None,) + a.shape[1:], lambda s, pt: (s, 0, 0))
    small = lambda a: pl.BlockSpec(a.shape, lambda s, pt: (0, 0))

    def page_spec(j):
        return pl.BlockSpec((None, page, aw), lambda s, pt: (pt[s, j], 0, 0))

    grid_spec = pltpu.PrefetchScalarGridSpec(
        num_scalar_prefetch=1,
        grid=(db,),
        in_specs=[seq(qrep), seq(kn), seq(vn)] + [small(a) for a in lams] + [small(subln_g4)]
        + [page_spec(j) for j in range(n_pages)] * 2,
        out_specs=pl.BlockSpec((None, s_new, aw), lambda s, pt: (s, 0, 0)),
    )
    return pl.pallas_call(
        functools.partial(_sample_attn_kernel, n_pages=n_pages, s_new=s_new, lam_init=lam_init),
        grid_spec=grid_spec,
        out_shape=jax.ShapeDtypeStruct((db, s_new, aw), F32),
        compiler_params=_cparams("arbitrary"),
        name="sample_attention",
    )(page_table, qrep, kn, vn, *lams, subln_g4, *([ck] * n_pages), *([cv] * n_pages))


def _pool_project(means, p, wp_ref, bp_ref, sc_ref):
    out = []
    for g in range(len(POOL_WINDOWS)):
        d = (means[g] - p[g]).astype(BF16)
        y = _dot(d, wp_ref[g]) + bp_ref[pl.ds(g, 1), :]
        out.append(y * sc_ref[pl.ds(g, 1), :])
    return out


def _pool_prompt_kernel(p_ref, prev_ref, wp_ref, bp_ref, sc_ref, o_ref, ext_ref, *, rb):
    j = pl.program_id(1)
    hist = prev_ref.shape[0]
    ext_ref[pl.ds(0, hist), :] = jnp.where(j == 0, 0.0, prev_ref[...])
    ext_ref[pl.ds(hist, rb), :] = p_ref[...]
    pos = j * rb + lax.broadcasted_iota(jnp.int32, (rb, POOL_GROUP_DIM), 0)
    means, cur = [], []
    for g, w in enumerate(POOL_WINDOWS):
        lanes = pl.ds(g * POOL_GROUP_DIM, POOL_GROUP_DIM)
        tot = ext_ref[pl.ds(hist, rb), lanes]
        cur.append(tot)
        for dlt in range(1, w):
            tot = tot + ext_ref[pl.ds(hist - dlt, rb), lanes]
        means.append(tot / jnp.minimum(pos + 1, w).astype(F32))
    ys = _pool_project(means, cur, wp_ref, bp_ref, sc_ref)
    o_ref[...] = jnp.concatenate(ys, axis=1).astype(BF16)


def _pool_prompt(p, wp, bp, sc, rb):
    nb, tp, pw = p.shape
    hist = 2 * SUBLANES
    row = pl.BlockSpec((None, rb, pw), lambda b, j: (b, j, 0))
    prev = pl.BlockSpec((None, hist, pw), lambda b, j: (b, jnp.maximum(j * (rb // hist) - 1, 0), 0))
    return pl.pallas_call(
        functools.partial(_pool_prompt_kernel, rb=rb),
        grid=(nb, tp // rb),
        in_specs=[row, prev, pl.BlockSpec(wp.shape, lambda b, j: (0, 0, 0)),
                  pl.BlockSpec(bp.shape, lambda b, j: (0, 0)), pl.BlockSpec(sc.shape, lambda b, j: (0, 0))],
        out_specs=row,
        out_shape=jax.ShapeDtypeStruct((nb, tp, pw), BF16),
        scratch_shapes=[pltpu.VMEM((hist + rb, pw), F32)],
        compiler_params=_cparams("parallel", "arbitrary"),
        name="pool_prompt",
    )(p, p, wp, bp, sc)


def _pool_sample_kernel(st_ref, ps_ref, wp_ref, bp_ref, sc_ref, o_ref, *, n_past):
    s_new = ps_ref.shape[0]
    rows = [st_ref[r] for r in range(POOL_STATE)] + [ps_ref[r] for r in range(s_new)]
    for qi in range(s_new):
        means, cur = [], []
        for g, w in enumerate(POOL_WINDOWS):
            lanes = slice(g * POOL_GROUP_DIM, (g + 1) * POOL_GROUP_DIM)
            tot = rows[POOL_STATE + qi][:, lanes]
            cur.append(tot)
            for dlt in range(1, w):
                tot = tot + rows[POOL_STATE + qi - dlt][:, lanes]
            means.append(tot / float(min(n_past + qi + 1, w)))
        ys = _pool_project(means, cur, wp_ref, bp_ref, sc_ref)
        o_ref[qi] = jnp.concatenate(ys, axis=1).astype(BF16)


def _pool_sample(st_t, ps_t, wp, bp, sc, n_past):
    s_new, db, pw = ps_t.shape
    return pl.pallas_call(
        functools.partial(_pool_sample_kernel, n_past=n_past),
        out_shape=jax.ShapeDtypeStruct((s_new, db, pw), BF16),
        compiler_params=pltpu.CompilerParams(vmem_limit_bytes=VMEM_LIMIT_BYTES),
        name="pool_sample",
    )(st_t, ps_t, wp, bp, sc)


def _top_values(x, count):
    vals = []
    for r in range(count):
        m = jnp.max(x, axis=0, keepdims=True)
        vals.append(m)
        if r + 1 < count:
            x = jnp.where(x == m, -jnp.inf, x)
    return vals


def _rank_pairs(limit):
    return [(r, c) for r in range(limit) for c in range(limit) if (r + 1) * (c + 1) <= limit]


def _mid_kernel(att_ref, pool_ref, x_ref, woa_ref, wop_ref, g2_ref, wq_ref, sk_ref,
                x1_ref, h2t_ref, s1_ref, s2_ref, th_ref, cand_ref):
    x1 = x_ref[...] + _dot(att_ref[...], woa_ref[...]) + _dot(pool_ref[...], wop_ref[...])
    x1_ref[...] = x1
    ms = jnp.mean(x1 * x1, axis=-1, keepdims=True)
    h2f = x1 * lax.rsqrt(ms + EPS) * g2_ref[...]
    h2 = h2f.astype(BF16)
    h2t_ref[...] = h2f.T.astype(BF16)
    qp = _dot(h2, wq_ref[...])
    pairs = _rank_pairs(PEER_TOPK + 1)
    cand_ref[...] = jnp.full(cand_ref.shape, -jnp.inf, F32)
    for hd in range(PEER_HEADS):
        st = []
        tops = []
        for c in range(2):
            hc = 2 * hd + c
            qhc = qp[:, hc * PEER_HALF:(hc + 1) * PEER_HALF].astype(BF16)
            s = _dot_t(sk_ref[hc], qhc)
            st.append(s)
            tops.append(_top_values(s, PEER_TOPK + 1))
        for n, (r, c) in enumerate(pairs):
            cand_ref[pl.ds(n, 1), :] = tops[0][r] + tops[1][c]
        best = _top_values(cand_ref[...], PEER_TOPK + 1)
        z = jnp.zeros_like(best[0])
        for r in range(PEER_TOPK):
            z = z + jnp.exp(best[r] - best[0])
        shift = -best[0] - jnp.log(z)
        s1_ref[hd] = st[0]
        s2_ref[hd] = st[1] + shift
        th_ref[pl.ds(hd, 1), :] = 0.5 * (best[PEER_TOPK - 1] + best[PEER_TOPK]) + shift


def _mid(att, pool, x, woa, wop, g2, wq, sk, rb):
    n, d = x.shape
    aw = att.shape[1]
    row = lambda w: pl.BlockSpec((rb, w), lambda i: (i, 0))
    full2 = lambda a: pl.BlockSpec(a.shape, lambda i: (0, 0))
    col3 = pl.BlockSpec((PEER_HEADS, N_KEYS, rb), lambda i: (0, 0, i))
    return pl.pallas_call(
        _mid_kernel,
        grid=(n // rb,),
        in_specs=[row(aw), row(pool.shape[1]), row(d), full2(woa), full2(wop), full2(g2), full2(wq),
                  pl.BlockSpec(sk.shape, lambda i: (0, 0, 0))],
        out_specs=[row(d), pl.BlockSpec((d, rb), lambda i: (0, i)), col3, col3,
                   pl.BlockSpec((PEER_HEADS, rb), lambda i: (0, i))],
        out_shape=[jax.ShapeDtypeStruct((n, d), F32), jax.ShapeDtypeStruct((d, n), BF16),
                   jax.ShapeDtypeStruct((PEER_HEADS, N_KEYS, n), F32),
                   jax.ShapeDtypeStruct((PEER_HEADS, N_KEYS, n), F32),
                   jax.ShapeDtypeStruct((PEER_HEADS, n), F32)],
        scratch_shapes=[pltpu.VMEM((-(-len(_rank_pairs(PEER_TOPK + 1)) // SUBLANES) * SUBLANES, rb), F32)],
        compiler_params=_cparams("parallel"),
        name="mid",
    )(att, pool, x, woa, wop, g2, wq, sk)


def _peer_kernel(h2t_ref, s1_ref, s2_ref, th_ref, wu_ref, wvt_ref, x1_ref, y_ref,
                 at_ref, ht_ref, acc_ref, *, ec, tb):
    e = pl.program_id(1)

    @pl.when(e == 0)
    def _():
        acc_ref[...] = jnp.zeros(acc_ref.shape, F32)

    at_ref[...] = _dot(wu_ref[...], h2t_ref[...])
    n_i = ec // N_KEYS
    assert n_i % SUBLANES == 0

    def body(tl, carry):
        lanes = pl.ds(pl.multiple_of(tl * LANES, LANES), LANES)
        s1 = [s1_ref[hd, pl.ds(pl.multiple_of(e * n_i, SUBLANES), n_i), lanes] for hd in range(PEER_HEADS)]
        th = [th_ref[pl.ds(hd, 1), lanes] for hd in range(PEER_HEADS)]
        for ii in range(n_i):
            rows = pl.ds(ii * N_KEYS, N_KEYS)
            g = jnp.zeros((N_KEYS, LANES), F32)
            for hd in range(PEER_HEADS):
                w = s2_ref[hd, :, lanes] + s1[hd][ii:ii + 1, :]
                g = g + jnp.where(w >= th[hd], jnp.exp(w), 0.0)
            ht_ref[rows, lanes] = (g * _gelu(at_ref[rows, lanes])).astype(BF16)
        return carry

    lax.fori_loop(0, tb // LANES, body, 0)
    acc_ref[...] += _dot(wvt_ref[...], ht_ref[...])

    @pl.when(e == pl.num_programs(1) - 1)
    def _():
        y_ref[...] = x1_ref[...] + acc_ref[...].T


def _peer(h2t, s1, s2, th, wu, wvt, x1, tb, ec):
    d, n = h2t.shape
    n_exp = wu.shape[0]
    tok3 = pl.BlockSpec((PEER_HEADS, N_KEYS, tb), lambda t, e: (0, 0, t))
    return pl.pallas_call(
        functools.partial(_peer_kernel, ec=ec, tb=tb),
        grid=(n // tb, n_exp // ec),
        in_specs=[pl.BlockSpec((d, tb), lambda t, e: (0, t)), tok3, tok3,
                  pl.BlockSpec((PEER_HEADS, tb), lambda t, e: (0, t)),
                  pl.BlockSpec((ec, d), lambda t, e: (e, 0)),
                  pl.BlockSpec((d, ec), lambda t, e: (0, e)),
                  pl.BlockSpec((tb, d), lambda t, e: (t, 0))],
        out_specs=pl.BlockSpec((tb, d), lambda t, e: (t, 0)),
        out_shape=jax.ShapeDtypeStruct((n, d), F32),
        scratch_shapes=[pltpu.VMEM((ec, tb), F32), pltpu.VMEM((ec, tb), BF16), pltpu.VMEM((d, tb), F32)],
        compiler_params=_cparams("parallel", "arbitrary"),
        name="peer",
    )(h2t, s1, s2, th, wu, wvt, x1)


def _largest_divisor(n, cap, mult):
    best = mult
    for c in range(mult, cap + 1, mult):
        if n % c == 0:
            best = c
    return best


def _plan(t_seq, n_batch, n_sample_rows):
    tq = MXU_DIM
    t_pad = -(-t_seq // tq) * tq
    n_all = n_batch * t_pad + n_sample_rows
    assert n_sample_rows % LANES == 0
    return dict(
        tq=tq, t_pad=t_pad, n_all=n_all,
        rb_seq=_largest_divisor(t_pad, 1024, tq),
        rb_all=_largest_divisor(n_all, 512, LANES),
        tb=_largest_divisor(n_all, 512, LANES),
    )


def kernel(x_prompt, x_sample, cache_k, cache_v, state_pool, page_table, meta_tokens, norm1_g, w_in,
           q_norm_g, k_norm_g, lambda_q1, lambda_k1, lambda_q2, lambda_k2, subln_g, w_pool, b_pool,
           pool_scale, w_out, norm2_g, w_query, sub_keys, w_u, w_v):
    depth = w_in.shape[0]
    assert depth == 1, "single-layer trunk"
    l = 0
    nb, seq, d = x_prompt.shape
    db, s_new, _ = x_sample.shape
    n_pages = page_table.shape[1]
    page = cache_k.shape[2]
    n_past = n_pages * page
    t_seq = seq + N_META
    aw = N_ATT_HEADS * HEAD_DIM
    pw = len(POOL_WINDOWS) * POOL_GROUP_DIM
    plan = _plan(t_seq, nb, db * s_new)
    t_pad, tq = plan["t_pad"], plan["tq"]
    lam_init = 0.8 - 0.6 * math.exp(-0.3 * l)

    win = w_in[l].astype(BF16)
    g1 = norm1_g[l][None, :]
    gq = jnp.tile(q_norm_g[l], aw // QK_DIM)[None, :]
    gk = jnp.tile(k_norm_g[l], aw // QK_DIM)[None, :]
    grp = np.arange(aw) // QK_DIM
    bd = jnp.asarray((grp[:, None] == grp[None, :]) / QK_DIM, BF16)
    lams = [a[l][None, :] for a in (lambda_q1, lambda_k1, lambda_q2, lambda_k2)]
    sg = subln_g[l][None, :]
    sg4 = jnp.tile(subln_g[l], N_ATT_HEADS)[None, :]
    wp = w_pool[l].astype(BF16)
    bp = b_pool[l]
    sc = pool_scale[l].reshape(len(POOL_WINDOWS), POOL_GROUP_DIM)
    woa = w_out[l][:aw].astype(BF16)
    wop = w_out[l][aw:].astype(BF16)
    g2 = norm2_g[l][None, :]
    wq = w_query[l].astype(BF16)
    sk = sub_keys[l].reshape(PEER_HEADS * 2, N_KEYS, PEER_HALF).astype(BF16)
    wu = w_u[l].astype(BF16)
    wvt = w_v[l].T.astype(BF16)

    xp = jnp.concatenate([jnp.broadcast_to(meta_tokens[None], (nb, N_META, d)), x_prompt,
                          jnp.zeros((nb, t_pad - t_seq, d), F32)], axis=1)
    tabs_p = _rope_tables(np.arange(t_pad))
    pk, pv, q0, q1, kb, vb, pp = _project(xp, t_seq, plan["rb_seq"], tabs_p, g1, win, bd, gq, gk)
    att_p = _flash(q0, q1, kb, vb, lams, sg, tq, lam_init)
    pool_p = _pool_prompt(pp, wp, bp, sc, plan["rb_seq"])

    xs = x_sample.reshape(1, db * s_new, d)
    tabs_s = _rope_tables(np.tile(n_past + np.arange(s_new), db))
    sk_new, sv_new, sq0, sq1, skb, svb, sp = _project(xs, db * s_new, db * s_new, tabs_s, g1, win, bd, gq, gk)
    qs = (sq0 + sq1).reshape(db, 1, s_new, aw)
    qrep = jnp.broadcast_to(qs, (db, 2 * N_ATT_HEADS, s_new, aw)).reshape(db, 2 * N_ATT_HEADS * s_new, aw)
    ck = cache_k[l].reshape(-1, page, aw)
    cv = cache_v[l].reshape(-1, page, aw)
    new_pad = ((0, 0), (0, -s_new % (2 * SUBLANES)), (0, 0))
    att_s = _sample_attention(qrep, jnp.pad(skb.reshape(db, s_new, aw), new_pad),
                              jnp.pad(svb.reshape(db, s_new, aw), new_pad), ck, cv,
                              page_table, lams, sg4, s_new, lam_init).astype(BF16)
    st = state_pool[l]
    sp3 = sp.reshape(db, s_new, pw)
    pool_s = _pool_sample(jnp.swapaxes(st, 0, 1), jnp.swapaxes(sp3, 0, 1), wp, bp, sc, n_past)
    pool_s = jnp.swapaxes(pool_s, 0, 1)

    x_all = jnp.concatenate([xp.reshape(nb * t_pad, d), xs[0]], axis=0)
    att_all = jnp.concatenate([att_p.reshape(nb * t_pad, aw), att_s.reshape(db * s_new, aw)], axis=0)
    pool_all = jnp.concatenate([pool_p.reshape(nb * t_pad, pw), pool_s.reshape(db * s_new, pw)], axis=0)
    x1, h2t, s1, s2, th = _mid(att_all, pool_all, x_all, woa, wop, g2, wq, sk, plan["rb_all"])
    ec = 1024
    y = _peer(h2t, s1, s2, th, wu, wvt, x1, plan["tb"], ec)

    y_prompt = y[:nb * t_pad].reshape(nb, t_pad, d)[:, N_META:t_seq]
    y_sample = y[nb * t_pad:].reshape(db, s_new, d)
    hshape = (N_ATT_HEADS, HEAD_DIM)
    prompt_pool = pp[:, t_seq - POOL_STATE:t_seq]
    sample_pool = jnp.concatenate([st, sp3], axis=1)[:, -POOL_STATE:]
    return (y_prompt, y_sample,
            pk.reshape(1, nb, t_seq, *hshape), pv.reshape(1, nb, t_seq, *hshape), prompt_pool[None],
            sk_new.reshape(1, db, s_new, *hshape), sv_new.reshape(1, db, s_new, *hshape), sample_pool[None])
```

```python
import functools
import math

import numpy as np
import jax
import jax.numpy as jnp
from jax import lax
from jax.experimental import pallas as pl
from jax.experimental.pallas import tpu as pltpu

F32 = jnp.float32
BF16 = jnp.bfloat16

N_META = 16
N_ATT_HEADS = 4
HEAD_DIM = 128
QK_DIM = 64
ROPE_DIM = 16
ROPE_THETA = 500000.0
POOL_WINDOWS = (2, 4, 8, 16)
POOL_GROUP_DIM = 128
POOL_STATE = 15
N_KEYS = 128
PEER_HEADS = 8
PEER_HALF = 128
PEER_TOPK = 16
EPS = 1e-6
NEG = -1e30
LOG2E = math.log2(math.e)

LANES = 128
SUBLANES = 8
MXU_DIM = 256
VMEM_LIMIT_BYTES = 56 * 1024 * 1024

_ERF_ALPHA = (-2.72614225801306e-10, 2.77068142495902e-08, -2.10102402082508e-06,
              -5.69250639462346e-05, -7.34990630326855e-04, -2.95459980854025e-03,
              -1.60960333262415e-02)
_ERF_BETA = (-1.45660718464996e-05, -2.13374055278905e-04, -1.68282697438203e-03,
             -7.37332916720468e-03, -1.42647390514189e-02)


def _erf(x):
    x = jnp.clip(x, -4.0, 4.0)
    x2 = x * x
    num = jnp.full_like(x, _ERF_ALPHA[0])
    for c in _ERF_ALPHA[1:]:
        num = num * x2 + c
    den = jnp.full_like(x, _ERF_BETA[0])
    for c in _ERF_BETA[1:]:
        den = den * x2 + c
    return x * num / den


def _gelu(x):
    return 0.5 * x * (1.0 + _erf(x * (2.0 ** -0.5)))


def _cparams(*sem):
    return pltpu.CompilerParams(dimension_semantics=sem, vmem_limit_bytes=VMEM_LIMIT_BYTES)


def _dot(a, b):
    return jnp.dot(a, b, preferred_element_type=F32)


def _dot_t(a, b):
    return lax.dot_general(a, b, (((1,), (1,)), ((), ())), preferred_element_type=F32)


def _rope_tables(positions):
    pos = np.asarray(positions, np.float64)[:, None]
    half = ROPE_DIM // 2
    inv = ROPE_THETA ** (-np.arange(0, ROPE_DIM, 2, dtype=np.float64) / ROPE_DIM)
    ang = pos * inv[None, :]
    cos, sin = np.cos(ang), np.sin(ang)
    n = pos.shape[0]
    c = np.ones((n, LANES)); s1 = np.zeros((n, LANES)); s2 = np.zeros((n, LANES))
    for comp in range(LANES // QK_DIM):
        o = comp * QK_DIM
        c[:, o:o + half] = cos; c[:, o + half:o + ROPE_DIM] = cos
        s2[:, o:o + half] = -sin
        s1[:, o + half:o + ROPE_DIM] = sin
    return (jnp.asarray(c, F32), jnp.asarray(s1, F32), jnp.asarray(s2, F32))


def _proj_kernel(x_ref, g1_ref, win_ref, bd_ref, gq_ref, gk_ref, c_ref, s1_ref, s2_ref,
                 k_ref, v_ref, q0_ref, q1_ref, kb_ref, vb_ref, p_ref):
    aw = N_ATT_HEADS * HEAD_DIM
    x = x_ref[...]
    ms = jnp.mean(x * x, axis=-1, keepdims=True)
    h = (x * lax.rsqrt(ms + EPS) * g1_ref[...]).astype(BF16)
    z = _dot(h, win_ref[...])
    c = jnp.concatenate([c_ref[...]] * N_ATT_HEADS, axis=1)
    s1 = jnp.concatenate([s1_ref[...]] * N_ATT_HEADS, axis=1)
    s2 = jnp.concatenate([s2_ref[...]] * N_ATT_HEADS, axis=1)
    bd = bd_ref[...]

    def qk_norm_rope(u, g):
        sq = u * u
        hi = sq.astype(BF16)
        lo = (sq - hi.astype(F32)).astype(BF16)
        msq = _dot(hi, bd) + _dot(lo, bd)
        un = u * lax.rsqrt(msq + EPS) * g
        return un * c + pltpu.roll(un, ROPE_DIM // 2, 1) * s1 + pltpu.roll(un, aw - ROPE_DIM // 2, 1) * s2

    q = qk_norm_rope(z[:, :aw], gq_ref[...]) * (QK_DIM ** -0.5)
    k = qk_norm_rope(z[:, aw:2 * aw], gk_ref[...])
    v = z[:, 2 * aw:3 * aw]
    lane = lax.broadcasted_iota(jnp.int32, q.shape, 1)
    first = (lane % (2 * QK_DIM)) < QK_DIM
    q0_ref[...] = jnp.where(first, q, 0.0).astype(BF16)
    q1_ref[...] = jnp.where(first, 0.0, q).astype(BF16)
    k_ref[...] = k
    v_ref[...] = v
    kb_ref[...] = k.astype(BF16)
    vb_ref[...] = v.astype(BF16)
    p_ref[...] = z[:, 3 * aw:]


def _project(x3, n_valid, rb, tabs, g1, win, bd, gq, gk):
    nb, r, d = x3.shape
    aw = N_ATT_HEADS * HEAD_DIM
    in_w = win.shape[1]
    row = lambda b, j: (b, j, 0)
    fixed = lambda b, j: (0, 0)
    blk = lambda w: pl.BlockSpec((None, rb, w), row)
    full = lambda a: pl.BlockSpec(a.shape, fixed)
    tab = pl.BlockSpec((rb, LANES), lambda b, j: (j, 0))
    return pl.pallas_call(
        _proj_kernel,
        grid=(nb, r // rb),
        in_specs=[blk(d), full(g1), full(win), full(bd), full(gq), full(gk), tab, tab, tab],
        out_specs=[blk(aw)] * 7,
        out_shape=[jax.ShapeDtypeStruct((nb, n_valid, aw), F32)] * 2
        + [jax.ShapeDtypeStruct((nb, r, aw), BF16)] * 4
        + [jax.ShapeDtypeStruct((nb, r, in_w - 3 * aw), F32)],
        compiler_params=_cparams("parallel", "arbitrary"),
        name="proj",
    )(x3, g1, win, bd, gq, gk, *tabs)


def _lambda(lq1, lk1, lq2, lk2, lam_init):
    a = jnp.sum(lq1[...] * lk1[...], axis=-1, keepdims=True)
    b = jnp.sum(lq2[...] * lk2[...], axis=-1, keepdims=True)
    return jnp.exp(a) - jnp.exp(b) + lam_init


def _subln(att, g, lam_init):
    ms = jnp.mean(att * att, axis=-1, keepdims=True)
    return att * lax.rsqrt(ms + EPS) * g * (1.0 - lam_init)


def _flash_kernel(q0_ref, q1_ref, k_ref, v_ref, lq1, lk1, lq2, lk2, g_ref, o_ref, m_ref, acc_ref,
                  *, tq, lam_init):
    qi = pl.program_id(1)
    heads = [slice(h * HEAD_DIM, (h + 1) * HEAD_DIM) for h in range(N_ATT_HEADS)]
    qs = [jnp.concatenate([q0_ref[:, hl], q1_ref[:, hl]], axis=0) for hl in heads]
    m_ref[...] = jnp.full(m_ref.shape, NEG, F32)
    acc_ref[...] = jnp.zeros(acc_ref.shape, F32)
    ones = jnp.ones((tq, HEAD_DIM), BF16)

    def step(j, masked):
        start = pl.multiple_of(j * tq, tq)
        for h, hl in enumerate(heads):
            kc = k_ref[pl.ds(start, tq), hl]
            vc = jnp.concatenate([v_ref[pl.ds(start, tq), hl], ones], axis=1)
            s = _dot_t(qs[h], kc)
            if masked:
                row = lax.broadcasted_iota(jnp.int32, s.shape, 0) % tq
                col = lax.broadcasted_iota(jnp.int32, s.shape, 1)
                s = jnp.where(col <= row, s, NEG)
            m_prev = m_ref[h]
            m_new = jnp.maximum(m_prev, jnp.max(s, axis=1, keepdims=True))
            alpha = jnp.exp(m_prev - m_new)
            p = jnp.exp(s - jnp.concatenate([m_new] * (tq // LANES), axis=1))
            acc_ref[h] = acc_ref[h] * jnp.concatenate([alpha] * 2, axis=1) + _dot(p.astype(BF16), vc)
            m_ref[h] = m_new

    def body(j, carry):
        step(j, False)
        return carry

    lax.fori_loop(0, qi, body, 0)
    step(qi, True)

    lam = _lambda(lq1, lk1, lq2, lk2, lam_init)
    for h, hl in enumerate(heads):
        acc = acc_ref[h]
        o = acc[:, :HEAD_DIM] / acc[:, HEAD_DIM:]
        att = o[:tq] - lam * o[tq:]
        o_ref[:, hl] = _subln(att, g_ref[...], lam_init).astype(BF16)


def _flash(q0, q1, kb, vb, lams, subln_g, tq, lam_init):
    nb, tp, aw = q0.shape
    qspec = pl.BlockSpec((None, tq, aw), lambda b, i: (b, i, 0))
    kspec = pl.BlockSpec((None, tp, aw), lambda b, i: (b, 0, 0))
    small = lambda a: pl.BlockSpec(a.shape, lambda b, i: (0, 0))
    return pl.pallas_call(
        functools.partial(_flash_kernel, tq=tq, lam_init=lam_init),
        grid=(nb, tp // tq),
        in_specs=[qspec, qspec, kspec, kspec] + [small(a) for a in lams] + [small(subln_g)],
        out_specs=qspec,
        out_shape=jax.ShapeDtypeStruct((nb, tp, aw), BF16),
        scratch_shapes=[pltpu.VMEM((N_ATT_HEADS, 2 * tq, LANES), F32),
                        pltpu.VMEM((N_ATT_HEADS, 2 * tq, 2 * HEAD_DIM), F32)],
        compiler_params=_cparams("parallel", "arbitrary"),
        name="prompt_attention",
    )(q0, q1, kb, vb, *lams, subln_g)


def _sample_attn_kernel(pt_ref, *refs, n_pages, s_new, lam_init):
    del pt_ref
    q_ref, kn_ref, vn_ref, lq1, lk1, lq2, lk2, g_ref = refs[:8]
    kp = refs[8:8 + n_pages]
    vp = refs[8 + n_pages:8 + 2 * n_pages]
    o_ref = refs[8 + 2 * n_pages]
    nh = N_ATT_HEADS
    q = q_ref[...]
    nrow = q.shape[0]
    rpp = kp[0].shape[0]
    own = (lax.broadcasted_iota(jnp.int32, (nrow, rpp), 1) % nh
           == (lax.broadcasted_iota(jnp.int32, (nrow, rpp), 0) % (nh * s_new)) // s_new)
    s_list = [jnp.where(own, _dot_t(q, kp[j][...].astype(BF16)), NEG) for j in range(n_pages)]
    pad = jnp.zeros((LANES - kn_ref.shape[0], HEAD_DIM), BF16)
    kn = jnp.concatenate([kn_ref[...], pad], axis=0)
    vn = jnp.concatenate([vn_ref[...], pad], axis=0)
    r2 = lax.broadcasted_iota(jnp.int32, (nrow, LANES), 0)
    c2 = lax.broadcasted_iota(jnp.int32, (nrow, LANES), 1)
    own_new = jnp.where(c2 % nh == (r2 % (nh * s_new)) // s_new, c2 // nh, s_new) <= r2 % s_new
    s_new_t = jnp.where(own_new, _dot_t(q, kn), NEG)
    m = s_list[0]
    for s in s_list[1:]:
        m = jnp.maximum(m, s)
    m = jnp.maximum(jnp.max(m, axis=1, keepdims=True), jnp.max(s_new_t, axis=1, keepdims=True))
    psum = jnp.zeros((nrow, rpp), F32)
    acc = jnp.zeros((nrow, HEAD_DIM), F32)
    for j, s in enumerate(s_list):
        p = jnp.exp(s - m)
        psum = psum + p
        acc = acc + _dot(p.astype(BF16), vp[j][...].astype(BF16))
    p = jnp.exp(s_new_t - m)
    acc = acc + _dot(p.astype(BF16), vn)
    den = jnp.sum(psum, axis=1, keepdims=True) + jnp.sum(p, axis=1, keepdims=True)
    o = acc / den
    half = nh * s_new
    lam = _lambda(lq1, lk1, lq2, lk2, lam_init)
    att = o[:half] - lam * o[half:]
    o_ref[...] = _subln(att, g_ref[...], lam_init)


def _sample_attention(qall, kn, vn, ck, cv, page_table, lams, subln_g, s_new, lam_init):
    db, nrow, hd = qall.shape
    n_pages = page_table.shape[1]
    rpp = ck.shape[1]
    seq = lambda a: pl.BlockSpec((None,) + a.shape[1:], lambda s, pt: (s, 0, 0))
    small = lambda a: pl.BlockSpec(a.shape, lambda s, pt: (0, 0))

    def page_spec(j):
        return pl.BlockSpec((None, rpp, hd), lambda s, pt: (pt[s, j], 0, 0))

    grid_spec = pltpu.PrefetchScalarGridSpec(
        num_scalar_prefetch=1,
        grid=(db,),
        in_specs=[seq(qall), seq(kn), seq(vn)] + [small(a) for a in lams] + [small(subln_g)]
        + [page_spec(j) for j in range(n_pages)] * 2,
        out_specs=pl.BlockSpec((None, nrow // 2, hd), lambda s, pt: (s, 0, 0)),
    )
    return pl.pallas_call(
        functools.partial(_sample_attn_kernel, n_pages=n_pages, s_new=s_new, lam_init=lam_init),
        grid_spec=grid_spec,
        out_shape=jax.ShapeDtypeStruct((db, nrow // 2, hd), F32),
        compiler_params=_cparams("arbitrary"),
        name="sample_attention",
    )(page_table, qall, kn, vn, *lams, subln_g, *([ck] * n_pages), *([cv] * n_pages))


def _pool_project(means, p, wp_ref, bp_ref, sc_ref):
    out = []
    for g in range(len(POOL_WINDOWS)):
        d = (means[g] - p[g]).astype(BF16)
        y = _dot(d, wp_ref[g]) + bp_ref[pl.ds(g, 1), :]
        out.append(y * sc_ref[pl.ds(g, 1), :])
    return out


def _pool_prompt_kernel(p_ref, prev_ref, wp_ref, bp_ref, sc_ref, o_ref, ext_ref, *, rb):
    j = pl.program_id(1)
    hist = prev_ref.shape[0]
    ext_ref[pl.ds(0, hist), :] = jnp.where(j == 0, 0.0, prev_ref[...])
    ext_ref[pl.ds(hist, rb), :] = p_ref[...]
    pos = j * rb + lax.broadcasted_iota(jnp.int32, (rb, POOL_GROUP_DIM), 0)
    means, cur = [], []
    for g, w in enumerate(POOL_WINDOWS):
        lanes = pl.ds(g * POOL_GROUP_DIM, POOL_GROUP_DIM)
        tot = ext_ref[pl.ds(hist, rb), lanes]
        cur.append(tot)
        for dlt in range(1, w):
            tot = tot + ext_ref[pl.ds(hist - dlt, rb), lanes]
        means.append(tot / jnp.minimum(pos + 1, w).astype(F32))
    ys = _pool_project(means, cur, wp_ref, bp_ref, sc_ref)
    o_ref[...] = jnp.concatenate(ys, axis=1).astype(BF16)


def _pool_prompt(p, wp, bp, sc, rb):
    nb, tp, pw = p.shape
    hist = 2 * SUBLANES
    row = pl.BlockSpec((None, rb, pw), lambda b, j: (b, j, 0))
    prev = pl.BlockSpec((None, hist, pw), lambda b, j: (b, jnp.maximum(j * (rb // hist) - 1, 0), 0))
    return pl.pallas_call(
        functools.partial(_pool_prompt_kernel, rb=rb),
        grid=(nb, tp // rb),
        in_specs=[row, prev, pl.BlockSpec(wp.shape, lambda b, j: (0, 0, 0)),
                  pl.BlockSpec(bp.shape, lambda b, j: (0, 0)), pl.BlockSpec(sc.shape, lambda b, j: (0, 0))],
        out_specs=row,
        out_shape=jax.ShapeDtypeStruct((nb, tp, pw), BF16),
        scratch_shapes=[pltpu.VMEM((hist + rb, pw), F32)],
        compiler_params=_cparams("parallel", "arbitrary"),
        name="pool_prompt",
    )(p, p, wp, bp, sc)


def _pool_sample_kernel(st_ref, ps_ref, wp_ref, bp_ref, sc_ref, o_ref, *, n_past):
    s_new = ps_ref.shape[0]
    rows = [st_ref[r] for r in range(POOL_STATE)] + [ps_ref[r] for r in range(s_new)]
    for qi in range(s_new):
        means, cur = [], []
        for g, w in enumerate(POOL_WINDOWS):
            lanes = slice(g * POOL_GROUP_DIM, (g + 1) * POOL_GROUP_DIM)
            tot = rows[POOL_STATE + qi][:, lanes]
            cur.append(tot)
            for dlt in range(1, w):
                tot = tot + rows[POOL_STATE + qi - dlt][:, lanes]
            means.append(tot / float(min(n_past + qi + 1, w)))
        ys = _pool_project(means, cur, wp_ref, bp_ref, sc_ref)
        o_ref[qi] = jnp.concatenate(ys, axis=1).astype(BF16)


def _pool_sample(st_t, ps_t, wp, bp, sc, n_past):
    s_new, db, pw = ps_t.shape
    return pl.pallas_call(
        functools.partial(_pool_sample_kernel, n_past=n_past),
        out_shape=jax.ShapeDtypeStruct((s_new, db, pw), BF16),
        compiler_params=pltpu.CompilerParams(vmem_limit_bytes=VMEM_LIMIT_BYTES),
        name="pool_sample",
    )(st_t, ps_t, wp, bp, sc)


def _top_values(x, count):
    vals = []
    for r in range(count):
        m = jnp.max(x, axis=0, keepdims=True)
        vals.append(m)
        if r + 1 < count:
            x = jnp.where(x == m, -jnp.inf, x)
    return vals


def _rank_pairs(limit):
    return [(r, c) for r in range(limit) for c in range(limit) if (r + 1) * (c + 1) <= limit]


def _mid_kernel(att_ref, pool_ref, x_ref, woa_ref, wop_ref, g2_ref, wq_ref, sk_ref,
                x1_ref, h2t_ref, s1_ref, s2_ref, th_ref, cand_ref):
    x1 = x_ref[...] + _dot(att_ref[...], woa_ref[...]) + _dot(pool_ref[...], wop_ref[...])
    x1_ref[...] = x1
    ms = jnp.mean(x1 * x1, axis=-1, keepdims=True)
    h2f = x1 * lax.rsqrt(ms + EPS) * g2_ref[...]
    h2 = h2f.astype(BF16)
    h2t_ref[...] = h2f.T.astype(BF16)
    qp = _dot(h2, wq_ref[...])
    pairs = _rank_pairs(PEER_TOPK + 1)
    cand_ref[...] = jnp.full(cand_ref.shape, -jnp.inf, F32)
    for hd in range(PEER_HEADS):
        st = []
        tops = []
        for c in range(2):
            hc = 2 * hd + c
            qhc = qp[:, hc * PEER_HALF:(hc + 1) * PEER_HALF].astype(BF16)
            s = _dot_t(sk_ref[hc], qhc)
            st.append(s)
            tops.append(_top_values(s, PEER_TOPK + 1))
        for n, (r, c) in enumerate(pairs):
            cand_ref[pl.ds(n, 1), :] = tops[0][r] + tops[1][c]
        best = _top_values(cand_ref[...], PEER_TOPK + 1)
        z = jnp.zeros_like(best[0])
        for r in range(PEER_TOPK):
            z = z + jnp.exp(best[r] - best[0])
        shift = -best[0] - jnp.log(z)
        s1_ref[hd] = st[0] * LOG2E
        s2_ref[hd] = (st[1] + shift) * LOG2E
        th_ref[pl.ds(hd, 1), :] = (0.5 * (best[PEER_TOPK - 1] + best[PEER_TOPK]) + shift) * LOG2E


def _mid(att, pool, x, woa, wop, g2, wq, sk, rb):
    n, d = x.shape
    aw = att.shape[1]
    row = lambda w: pl.BlockSpec((rb, w), lambda i: (i, 0))
    full2 = lambda a: pl.BlockSpec(a.shape, lambda i: (0, 0))
    col3 = pl.BlockSpec((PEER_HEADS, N_KEYS, rb), lambda i: (0, 0, i))
    return pl.pallas_call(
        _mid_kernel,
        grid=(n // rb,),
        in_specs=[row(aw), row(pool.shape[1]), row(d), full2(woa), full2(wop), full2(g2), full2(wq),
                  pl.BlockSpec(sk.shape, lambda i: (0, 0, 0))],
        out_specs=[row(d), pl.BlockSpec((d, rb), lambda i: (0, i)), col3, col3,
                   pl.BlockSpec((PEER_HEADS, rb), lambda i: (0, i))],
        out_shape=[jax.ShapeDtypeStruct((n, d), F32), jax.ShapeDtypeStruct((d, n), BF16),
                   jax.ShapeDtypeStruct((PEER_HEADS, N_KEYS, n), F32),
                   jax.ShapeDtypeStruct((PEER_HEADS, N_KEYS, n), F32),
                   jax.ShapeDtypeStruct((PEER_HEADS, n), F32)],
        scratch_shapes=[pltpu.VMEM((-(-len(_rank_pairs(PEER_TOPK + 1)) // SUBLANES) * SUBLANES, rb), F32)],
        compiler_params=_cparams("parallel"),
        name="mid",
    )(att, pool, x, woa, wop, g2, wq, sk)


def _peer_kernel(h2t_ref, s1_ref, s2_ref, th_ref, wu_ref, wvt_ref, x1_ref, y_ref,
                 at0_ref, at1_ref, ht0_ref, ht1_ref, acc_ref, *, ec, tb, n_ec, n_items):
    k = pl.program_id(0)
    e1 = jnp.clip(k - 1, 0, n_items - 1) % n_ec
    e2 = jnp.clip(k - 2, 0, n_items - 1) % n_ec

    @pl.when(k == 0)
    def _():
        for ref in (at0_ref, at1_ref, ht0_ref, ht1_ref):
            ref[...] = jnp.zeros(ref.shape, ref.dtype)

    @pl.when(e2 == 0)
    def _():
        acc_ref[...] = jnp.zeros(acc_ref.shape, F32)

    n_i = ec // N_KEYS
    assert n_i % SUBLANES == 0

    def stages(at_w, at_r, ht_w, ht_r):
        def matmul_piece(n):
            rows = slice((n % n_i) * N_KEYS, (n % n_i + 1) * N_KEYS)
            if n < n_i:
                at_w[rows, :] = _dot(wu_ref[rows, :], h2t_ref[...])
            else:
                rows = slice((n - n_i) * LANES, (n - n_i + 1) * LANES)
                acc_ref[rows, :] += _dot(wvt_ref[rows, :], ht_r[...])

        n_pieces = n_i + acc_ref.shape[0] // LANES
        n_units = (tb // LANES) * n_i
        done = 0
        s1_rows = pl.ds(pl.multiple_of(e1 * n_i, SUBLANES), n_i)
        for tl in range(tb // LANES):
            lanes = slice(tl * LANES, (tl + 1) * LANES)
            s1 = [s1_ref[hd, s1_rows, lanes] for hd in range(PEER_HEADS)]
            th = [th_ref[pl.ds(hd, 1), lanes] for hd in range(PEER_HEADS)]
            for ii in range(n_i):
                rows = slice(ii * N_KEYS, (ii + 1) * N_KEYS)
                g = jnp.zeros((N_KEYS, LANES), F32)
                for hd in range(PEER_HEADS):
                    w = s2_ref[hd, :, lanes] + s1[hd][ii:ii + 1, :]
                    g = g + jnp.where(w >= th[hd], jnp.exp2(w), 0.0)
                ht_w[rows, lanes] = (g * _gelu(at_r[rows, lanes])).astype(BF16)
                want = ((tl * n_i + ii + 1) * n_pieces) // n_units
                for n in range(done, want):
                    matmul_piece(n)
                done = want

    @pl.when(k % 2 == 0)
    def _():
        stages(at0_ref, at1_ref, ht1_ref, ht0_ref)

    @pl.when(k % 2 == 1)
    def _():
        stages(at1_ref, at0_ref, ht0_ref, ht1_ref)

    @pl.when(jnp.logical_and(k >= 2, e2 == n_ec - 1))
    def _():
        y_ref[...] = x1_ref[...] + acc_ref[...].T


def _peer(h2t, s1, s2, th, wu, wvt, x1, tb, ec):
    d, n = h2t.shape
    n_ec = wu.shape[0] // ec
    n_items = (n // tb) * n_ec

    def tok(lag):
        return lambda k: jnp.clip(k - lag, 0, n_items - 1) // n_ec

    def chunk(lag):
        return lambda k: jnp.clip(k - lag, 0, n_items - 1) % n_ec

    tok3 = pl.BlockSpec((PEER_HEADS, N_KEYS, tb), lambda k: (0, 0, tok(1)(k)))
    return pl.pallas_call(
        functools.partial(_peer_kernel, ec=ec, tb=tb, n_ec=n_ec, n_items=n_items),
        grid=(n_items + 2,),
        in_specs=[pl.BlockSpec((d, tb), lambda k: (0, tok(0)(k))), tok3, tok3,
                  pl.BlockSpec((PEER_HEADS, tb), lambda k: (0, tok(1)(k))),
                  pl.BlockSpec((ec, d), lambda k: (chunk(0)(k), 0)),
                  pl.BlockSpec((d, ec), lambda k: (0, chunk(2)(k))),
                  pl.BlockSpec((tb, d), lambda k: (tok(2)(k), 0))],
        out_specs=pl.BlockSpec((tb, d), lambda k: (tok(2)(k), 0)),
        out_shape=jax.ShapeDtypeStruct((n, d), F32),
        scratch_shapes=[pltpu.VMEM((ec, tb), F32), pltpu.VMEM((ec, tb), F32),
                        pltpu.VMEM((ec, tb), BF16), pltpu.VMEM((ec, tb), BF16), pltpu.VMEM((d, tb), F32)],
        compiler_params=_cparams("arbitrary"),
        name="peer",
    )(h2t, s1, s2, th, wu, wvt, x1)


def _largest_divisor(n, cap, mult):
    best = mult
    for c in range(mult, cap + 1, mult):
        if n % c == 0:
            best = c
    return best


def _plan(t_seq, n_batch, n_sample_rows):
    tq = MXU_DIM
    t_pad = -(-t_seq // tq) * tq
    n_all = n_batch * t_pad + n_sample_rows
    assert n_sample_rows % LANES == 0
    return dict(
        tq=tq, t_pad=t_pad, n_all=n_all,
        rb_seq=_largest_divisor(t_pad, 1024, tq),
        rb_all=_largest_divisor(n_all, 512, LANES),
        tb=_largest_divisor(n_all, 512, LANES),
    )


def kernel(x_prompt, x_sample, cache_k, cache_v, state_pool, page_table, meta_tokens, norm1_g, w_in,
           q_norm_g, k_norm_g, lambda_q1, lambda_k1, lambda_q2, lambda_k2, subln_g, w_pool, b_pool,
           pool_scale, w_out, norm2_g, w_query, sub_keys, w_u, w_v):
    depth = w_in.shape[0]
    assert depth == 1, "single-layer trunk"
    l = 0
    nb, seq, d = x_prompt.shape
    db, s_new, _ = x_sample.shape
    n_pages = page_table.shape[1]
    page = cache_k.shape[2]
    n_past = n_pages * page
    t_seq = seq + N_META
    aw = N_ATT_HEADS * HEAD_DIM
    pw = len(POOL_WINDOWS) * POOL_GROUP_DIM
    plan = _plan(t_seq, nb, db * s_new)
    t_pad, tq = plan["t_pad"], plan["tq"]
    lam_init = 0.8 - 0.6 * math.exp(-0.3 * l)

    win = w_in[l].astype(BF16)
    g1 = norm1_g[l][None, :]
    gq = jnp.tile(q_norm_g[l], aw // QK_DIM)[None, :]
    gk = jnp.tile(k_norm_g[l], aw // QK_DIM)[None, :]
    grp = np.arange(aw) // QK_DIM
    bd = jnp.asarray((grp[:, None] == grp[None, :]) / QK_DIM, BF16)
    lams = [a[l][None, :] for a in (lambda_q1, lambda_k1, lambda_q2, lambda_k2)]
    sg = subln_g[l][None, :]
    wp = w_pool[l].astype(BF16)
    bp = b_pool[l]
    sc = pool_scale[l].reshape(len(POOL_WINDOWS), POOL_GROUP_DIM)
    woa = w_out[l][:aw].astype(BF16)
    wop = w_out[l][aw:].astype(BF16)
    g2 = norm2_g[l][None, :]
    wq = w_query[l].astype(BF16)
    sk = sub_keys[l].reshape(PEER_HEADS * 2, N_KEYS, PEER_HALF).astype(BF16)
    wu = w_u[l].astype(BF16)
    wvt = w_v[l].T.astype(BF16)

    xp = jnp.concatenate([jnp.broadcast_to(meta_tokens[None], (nb, N_META, d)), x_prompt,
                          jnp.zeros((nb, t_pad - t_seq, d), F32)], axis=1)
    tabs_p = _rope_tables(np.arange(t_pad))
    pk, pv, q0, q1, kb, vb, pp = _project(xp, t_seq, plan["rb_seq"], tabs_p, g1, win, bd, gq, gk)
    att_p = _flash(q0, q1, kb, vb, lams, sg, tq, lam_init)
    pool_p = _pool_prompt(pp, wp, bp, sc, plan["rb_seq"])

    xs = x_sample.reshape(1, db * s_new, d)
    tabs_s = _rope_tables(np.tile(n_past + np.arange(s_new), db))
    sk_new, sv_new, sq0, sq1, skb, svb, sp = _project(xs, db * s_new, db * s_new, tabs_s, g1, win, bd, gq, gk)
    by_head = lambda a: a.reshape(db, s_new, N_ATT_HEADS, HEAD_DIM).swapaxes(1, 2)
    qall = jnp.stack([by_head(sq0), by_head(sq1)], axis=1).reshape(db, 2 * N_ATT_HEADS * s_new, HEAD_DIM)
    ck = cache_k[l].reshape(-1, page * N_ATT_HEADS, HEAD_DIM)
    cv = cache_v[l].reshape(-1, page * N_ATT_HEADS, HEAD_DIM)
    new_rows = s_new * N_ATT_HEADS
    assert new_rows % (2 * SUBLANES) == 0 and new_rows <= LANES
    att_s = _sample_attention(qall, skb.reshape(db, new_rows, HEAD_DIM), svb.reshape(db, new_rows, HEAD_DIM),
                              ck, cv, page_table, lams, sg, s_new, lam_init)
    att_s = att_s.reshape(db, N_ATT_HEADS, s_new, HEAD_DIM).swapaxes(1, 2).astype(BF16)
    st = state_pool[l]
    sp3 = sp.reshape(db, s_new, pw)
    pool_s = _pool_sample(jnp.swapaxes(st, 0, 1), jnp.swapaxes(sp3, 0, 1), wp, bp, sc, n_past)
    pool_s = jnp.swapaxes(pool_s, 0, 1)

    x_all = jnp.concatenate([xp.reshape(nb * t_pad, d), xs[0]], axis=0)
    att_all = jnp.concatenate([att_p.reshape(nb * t_pad, aw), att_s.reshape(db * s_new, aw)], axis=0)
    pool_all = jnp.concatenate([pool_p.reshape(nb * t_pad, pw), pool_s.reshape(db * s_new, pw)], axis=0)
    x1, h2t, s1, s2, th = _mid(att_all, pool_all, x_all, woa, wop, g2, wq, sk, plan["rb_all"])
    ec = 1024
    y = _peer(h2t, s1, s2, th, wu, wvt, x1, plan["tb"], ec)

    y_prompt = y[:nb * t_pad].reshape(nb, t_pad, d)[:, N_META:t_seq]
    y_sample = y[nb * t_pad:].reshape(db, s_new, d)
    hshape = (N_ATT_HEADS, HEAD_DIM)
    prompt_pool = pp[:, t_seq - POOL_STATE:t_seq]
    sample_pool = jnp.concatenate([st, sp3], axis=1)[:, -POOL_STATE:]
    return (y_prompt, y_sample,
            pk.reshape(1, nb, t_seq, *hshape), pv.reshape(1, nb, t_seq, *hshape), prompt_pool[None],
            sk_new.reshape(1, db, s_new, *hshape), sv_new.reshape(1, db, s_new, *hshape), sample_pool[None])
```
